```python
import math
import jax, jax.numpy as jnp
from jax import lax
import numpy as np

D_MODEL = 1024
BATCH = 16
SEQ = 2048
DEPTH = 2
DEC_BATCH = 32
DEC_SEQ = 4
PAST_LEN = 16384
PAGE_SIZE = 128

GLA_HEADS = 4
GLA_DK = 32
GLA_DV = 64
GLA_GATE_RANK = 16
GLA_GATE_NORMALIZER = 16.0
HGRN_HEADS = 4
HGRN_DK = 64
HGRN_DV = 64
DIFF_HEADS = 8
DIFF_DQK = 32
DIFF_DV = 64
ROPE_THETA = 10000.0
GLA_QK = GLA_HEADS * GLA_DK
GLA_W = GLA_HEADS * GLA_DV
HGRN_K = HGRN_HEADS * HGRN_DK
HGRN_W = HGRN_HEADS * HGRN_DV
DIFF_QK = DIFF_HEADS * 2 * DIFF_DQK
DIFF_W = DIFF_HEADS * DIFF_DV
MIX_W = GLA_W + HGRN_W + DIFF_W
IN_W = 2 * GLA_QK + GLA_W + GLA_GATE_RANK + GLA_W + 2 * HGRN_K + 2 * HGRN_W + 2 * DIFF_QK + DIFF_W
D_FF = 4 * D_MODEL
PLE_DIM = 256
LIN_CHUNK = 16
Q_BLOCK = 128
EPS = 1e-6
MASK_VALUE = -1e30
MIN_F = 1e-30
POOL_NUM = 5
POOL_DEN = 4

kernel_name = "hymba_gla_hgrn2_diffattn_step"


def _rmsnorm(x, g):
    xf = x.astype(jnp.float32)
    y = xf * lax.rsqrt(jnp.mean(xf * xf, axis=-1, keepdims=True) + EPS) * g.astype(jnp.float32)
    return y.astype(x.dtype)


def _split_in(proj):
    sizes = (GLA_QK, GLA_QK, GLA_W, GLA_GATE_RANK, GLA_W,
             HGRN_K, HGRN_K, HGRN_W, HGRN_W,
             DIFF_QK, DIFF_QK, DIFF_W)
    idx = np.cumsum(np.array(sizes))[:-1].tolist()
    return jnp.split(proj, idx, axis=-1)


def _rope(x, pos):
    half = x.shape[-1] // 2
    inv = ROPE_THETA ** (-jnp.arange(half, dtype=jnp.float32) / half)
    ang = pos.astype(jnp.float32)[:, None] * inv[None, :]
    cos = jnp.cos(ang)[None, :, None, None, :]
    sin = jnp.sin(ang)[None, :, None, None, :]
    xf = x.astype(jnp.float32)
    x1, x2 = xf[..., :half], xf[..., half:]
    return jnp.concatenate([x1 * cos - x2 * sin, x2 * cos + x1 * sin], axis=-1).astype(x.dtype)


def _gated_linear_chunked(q, k, v, g, s0):
    B, T, H, DK = q.shape
    DV = v.shape[-1]
    C = LIN_CHUNK
    pad = (-T) % C
    f32 = jnp.float32
    padw = ((0, 0), (0, pad), (0, 0), (0, 0))
    q, k, v, g = [jnp.pad(a.astype(f32), padw) for a in (q, k, v, g)]
    N = (T + pad) // C
    q, k, v, g = [a.reshape(B, N, C, H, a.shape[-1]) for a in (q, k, v, g)]
    b = jnp.cumsum(g, axis=2)
    b_last = b[:, :, -1]
    causal = jnp.tril(jnp.ones((C, C), dtype=bool))
    rel = b[:, :, :, None] - b[:, :, None, :]
    decay = jnp.exp(jnp.where(causal[:, :, None, None], rel, MASK_VALUE))
    att = jnp.einsum('bnthd,bnshd,bntshd->bnhts', q, k, decay)
    o_intra = jnp.einsum('bnhts,bnshv->bnthv', att, v)
    k_dec = k * jnp.exp(b_last[:, :, None] - b)
    upd = jnp.einsum('bnshd,bnshv->bnhdv', k_dec, v)
    dec_chunk = jnp.exp(b_last)

    def step(S, inp):
        dc, u = inp
        return dc[..., None] * S + u, S

    s_final, s_before = lax.scan(step, s0.astype(f32),
                                 (jnp.moveaxis(dec_chunk, 1, 0), jnp.moveaxis(upd, 1, 0)))
    s_before = jnp.moveaxis(s_before, 0, 1)
    o_inter = jnp.einsum('bnthd,bnhdv->bnthv', q * jnp.exp(b), s_before)
    o = (o_intra + o_inter).reshape(B, N * C, H, DV)[:, :T]
    return o, s_final


def _diff_attn(q, k, v, q_pos, k_pos, lam):
    s = jnp.einsum('bthcd,bshcd->bhcts', q.astype(jnp.float32), k.astype(jnp.float32)) * (DIFF_DQK ** -0.5)
    s = jnp.where(k_pos[None, :] <= q_pos[:, None], s, MASK_VALUE)
    a = jax.nn.softmax(s, axis=-1)
    wgt = a[:, :, 0] - lam * a[:, :, 1]
    return jnp.einsum('bhts,bshv->bthv', wgt, v.astype(jnp.float32))


def _trunk(x, p, pos, gla_s0, hgrn_s0, attend, w):
    B, T, _ = x.shape
    lb_soft = jax.nn.softmax(w['hgrn_lb_logits'].astype(jnp.float32), axis=0)
    lower_bounds = jnp.cumsum(lb_soft, axis=0) - lb_soft[0]
    h = x
    gla_st, hgrn_st, k_rows, v_rows = [], [], [], []
    for l in range(DEPTH):
        xn = _rmsnorm(h, w['norm1'][l])
        (gq, gk, gv, gg, gr, hq, hf, hi, hg, dq, dk, dv) = _split_in(xn @ w['w_in'][l])
        glog = jax.nn.log_sigmoid((gg @ w['gla_wg2'][l] + w['gla_bg'][l]).astype(jnp.float32)) / GLA_GATE_NORMALIZER
        o_a, s_a = _gated_linear_chunked(
            gq.reshape(B, T, GLA_HEADS, GLA_DK) * (GLA_DK ** -0.5),
            gk.reshape(B, T, GLA_HEADS, GLA_DK),
            gv.reshape(B, T, GLA_HEADS, GLA_DV),
            glog.reshape(B, T, GLA_HEADS, GLA_DK), gla_s0[l])
        o_a = _rmsnorm(o_a.astype(x.dtype), w['gla_onorm'][l]) * jax.nn.silu(gr.reshape(B, T, GLA_HEADS, GLA_DV))
        lbl = lower_bounds[l].reshape(HGRN_HEADS, HGRN_DK)
        f = lbl + (1.0 - lbl) * jax.nn.sigmoid(hf.astype(jnp.float32).reshape(B, T, HGRN_HEADS, HGRN_DK))
        logf = jnp.log(jnp.maximum(f, MIN_F))
        one_minus_f = 1.0 - f
        o_b, s_b = _gated_linear_chunked(
            jax.nn.silu(hq.reshape(B, T, HGRN_HEADS, HGRN_DK)), one_minus_f,
            hi.reshape(B, T, HGRN_HEADS, HGRN_DV), logf, hgrn_s0[l])
        o_b = _rmsnorm(o_b.astype(x.dtype), w['hgrn_onorm'][l]) * jax.nn.silu(hg.reshape(B, T, HGRN_HEADS, HGRN_DV))
        qd = _rope(dq.reshape(B, T, DIFF_HEADS, 2, DIFF_DQK), pos)
        kd = _rope(dk.reshape(B, T, DIFF_HEADS, 2, DIFF_DQK), pos)
        vd = dv.reshape(B, T, DIFF_HEADS, DIFF_DV)
        lam_init = 0.8 - 0.6 * math.exp(-0.3 * l)
        lv = w['diff_lambda'][l].astype(jnp.float32)
        lam = jnp.exp(jnp.sum(lv[0] * lv[1])) - jnp.exp(jnp.sum(lv[2] * lv[3])) + lam_init
        o_c = attend(l, qd, kd, vd, lam)
        o_c = _rmsnorm(o_c, w['diff_subln'][l]) * (1.0 - lam_init)
        mix = jnp.concatenate([o_a.reshape(B, T, GLA_W), o_b.reshape(B, T, HGRN_W),
                               o_c.reshape(B, T, DIFF_W)], axis=-1)
        h = h + mix @ w['w_out'][l]
        hn = _rmsnorm(h, w['norm2'][l])
        h = h + jnp.square(jax.nn.relu(hn @ w['w_up'][l])) @ w['w_down'][l]
        h = h + jax.nn.sigmoid(h @ w['ple_gate'][l]) * (p[l] @ w['ple_proj'][l])
        gla_st.append(s_a)
        hgrn_st.append(s_b)
        k_rows.append(kd.reshape(B, T, DIFF_HEADS, 2 * DIFF_DQK))
        v_rows.append(vd)
    y = _rmsnorm(h, w['final_norm'])
    return y, jnp.stack(gla_st), jnp.stack(hgrn_st), jnp.stack(k_rows), jnp.stack(v_rows)


def setup_inputs(seed: int = 0) -> dict:
    key = jax.random.key(seed)
    ks = jax.random.split(key, 32)
    n_pages = PAST_LEN // PAGE_SIZE
    n_pool = (DEC_BATCH * n_pages * POOL_NUM) // POOL_DEN

    def nrm(k, shape, scale=1.0):
        return jax.random.normal(k, shape, jnp.float32) * scale

    perm = jax.random.permutation(ks[8], n_pool)[: DEC_BATCH * n_pages]
    page_table = perm.reshape(DEC_BATCH, n_pages).astype(jnp.int32)
    return {
        'x_prompt': nrm(ks[0], (BATCH, SEQ, D_MODEL)),
        'x_sample': nrm(ks[1], (DEC_BATCH, DEC_SEQ, D_MODEL)),
        'state_gla': nrm(ks[2], (DEPTH, DEC_BATCH, GLA_HEADS, GLA_DK, GLA_DV)),
        'state_hgrn': nrm(ks[3], (DEPTH, DEC_BATCH, HGRN_HEADS, HGRN_DK, HGRN_DV)),
        'cache_k': nrm(ks[4], (DEPTH, n_pool, PAGE_SIZE, DIFF_HEADS, 2 * DIFF_DQK)),
        'cache_v': nrm(ks[5], (DEPTH, n_pool, PAGE_SIZE, DIFF_HEADS, DIFF_DV)),
        'page_table': page_table,
        'p_prompt': nrm(ks[6], (DEPTH, BATCH, SEQ, PLE_DIM)),
        'p_sample': nrm(ks[7], (DEPTH, DEC_BATCH, DEC_SEQ, PLE_DIM)),
        'norm1': 1.0 + nrm(ks[9], (DEPTH, D_MODEL), 0.02),
        'w_in': nrm(ks[10], (DEPTH, D_MODEL, IN_W), D_MODEL ** -0.5),
        'gla_wg2': nrm(ks[11], (DEPTH, GLA_GATE_RANK, GLA_QK), GLA_GATE_RANK ** -0.5),
        'gla_bg': nrm(ks[12], (DEPTH, GLA_QK), 0.1),
        'gla_onorm': 1.0 + nrm(ks[13], (DEPTH, GLA_DV), 0.02),
        'hgrn_lb_logits': nrm(ks[14], (DEPTH, HGRN_K), 0.1),
        'hgrn_onorm': 1.0 + nrm(ks[15], (DEPTH, HGRN_DV), 0.02),
        'diff_lambda': nrm(ks[16], (DEPTH, 4, DIFF_DQK), 0.1),
        'diff_subln': 1.0 + nrm(ks[17], (DEPTH, DIFF_DV), 0.02),
        'w_out': nrm(ks[18], (DEPTH, MIX_W, D_MODEL), MIX_W ** -0.5),
        'norm2': 1.0 + nrm(ks[19], (DEPTH, D_MODEL), 0.02),
        'w_up': nrm(ks[20], (DEPTH, D_MODEL, D_FF), D_MODEL ** -0.5),
        'w_down': nrm(ks[21], (DEPTH, D_FF, D_MODEL), D_FF ** -0.5),
        'ple_proj': nrm(ks[22], (DEPTH, PLE_DIM, D_MODEL), PLE_DIM ** -0.5),
        'ple_gate': nrm(ks[23], (DEPTH, D_MODEL, D_MODEL), D_MODEL ** -0.5),
        'final_norm': 1.0 + nrm(ks[24], (D_MODEL,), 0.02),
    }


def reference(x_prompt, x_sample, state_gla, state_hgrn, cache_k, cache_v, page_table,
              p_prompt, p_sample, norm1, w_in, gla_wg2, gla_bg, gla_onorm, hgrn_lb_logits,
              hgrn_onorm, diff_lambda, diff_subln, w_out, norm2, w_up, w_down, ple_proj,
              ple_gate, final_norm):
    w = {'norm1': norm1, 'w_in': w_in, 'gla_wg2': gla_wg2, 'gla_bg': gla_bg,
         'gla_onorm': gla_onorm, 'hgrn_lb_logits': hgrn_lb_logits, 'hgrn_onorm': hgrn_onorm,
         'diff_lambda': diff_lambda, 'diff_subln': diff_subln, 'w_out': w_out, 'norm2': norm2,
         'w_up': w_up, 'w_down': w_down, 'ple_proj': ple_proj, 'ple_gate': ple_gate,
         'final_norm': final_norm}

    B, T = x_prompt.shape[:2]
    pos_p = jnp.arange(T)

    def attend_prompt(l, q, k, v, lam):
        nb = T // Q_BLOCK
        qb = jnp.swapaxes(q.reshape(B, nb, Q_BLOCK, DIFF_HEADS, 2, DIFF_DQK), 0, 1)
        qpos = pos_p.reshape(nb, Q_BLOCK)
        out = lax.map(lambda a: _diff_attn(a[0], k, v, a[1], pos_p, lam), (qb, qpos))
        return jnp.swapaxes(out, 0, 1).reshape(B, T, DIFF_HEADS, DIFF_DV).astype(v.dtype)

    gla0_p = jnp.zeros((DEPTH, B, GLA_HEADS, GLA_DK, GLA_DV), jnp.float32)
    hgrn0_p = jnp.zeros((DEPTH, B, HGRN_HEADS, HGRN_DK, HGRN_DV), jnp.float32)
    y_prompt, gla_p, hgrn_p, k_p, v_p = _trunk(x_prompt, p_prompt, pos_p, gla0_p, hgrn0_p,
                                                attend_prompt, w)

    Bd, Td = x_sample.shape[:2]
    n_pages = page_table.shape[1]
    past = n_pages * PAGE_SIZE
    pos_s = past + jnp.arange(Td)
    k_pos_s = jnp.arange(past + Td)

    def attend_sample(l, q, k, v, lam):
        kp = cache_k[l, page_table].reshape(Bd, past, DIFF_HEADS, 2, DIFF_DQK)
        vp = cache_v[l, page_table].reshape(Bd, past, DIFF_HEADS, DIFF_DV)
        k_all = jnp.concatenate([kp.astype(k.dtype), k], axis=1)
        v_all = jnp.concatenate([vp.astype(v.dtype), v], axis=1)
        return _diff_attn(q, k_all, v_all, pos_s, k_pos_s, lam).astype(v.dtype)

    y_sample, gla_s, hgrn_s, k_s, v_s = _trunk(x_sample, p_sample, pos_s, state_gla, state_hgrn,
                                                attend_sample, w)
    return (y_prompt, y_sample, gla_p, hgrn_p, k_p, v_p, gla_s, hgrn_s, k_s, v_s)
```

```python
import functools
import math

import jax
import jax.numpy as jnp
import numpy as np
from jax import lax
from jax.experimental import pallas as pl
from jax.experimental.pallas import tpu as pltpu

D_MODEL = 1024
DEPTH = 2
PAGE_SIZE = 128
GLA_HEADS = 4
GLA_DK = 32
GLA_DV = 64
GLA_GATE_RANK = 16
GLA_GATE_NORMALIZER = 16.0
HGRN_HEADS = 4
HGRN_DK = 64
HGRN_DV = 64
DIFF_HEADS = 8
DIFF_DQK = 32
DIFF_DV = 64
ROPE_THETA = 10000.0
GLA_QK = GLA_HEADS * GLA_DK
GLA_W = GLA_HEADS * GLA_DV
HGRN_K = HGRN_HEADS * HGRN_DK
HGRN_W = HGRN_HEADS * HGRN_DV
DIFF_QK = DIFF_HEADS * 2 * DIFF_DQK
DIFF_W = DIFF_HEADS * DIFF_DV
D_FF = 4 * D_MODEL
PLE_DIM = 256
EPS = 1e-6
MASK_VALUE = -1e30
MIN_F = 1e-30

LANES = 128
VMEM_LIMIT = 56 * 1024 * 1024

F32 = jnp.float32
BF16 = jnp.bfloat16

_SEG_SIZES = (("gq", GLA_QK), ("gk", GLA_QK), ("gv", GLA_W), ("gr", GLA_W),
              ("hq", HGRN_K), ("hf", HGRN_K), ("hi", HGRN_W), ("hg", HGRN_W),
              ("dq", DIFF_QK), ("dk", DIFF_QK), ("dv", DIFF_W), ("gg", LANES))
_SEG = {}
_off = 0
for _name, _size in _SEG_SIZES:
    _SEG[_name] = (_off, _off + _size)
    _off += _size
IN_W_PAD = _off
GLA_IN_W = 3 * GLA_QK + 2 * GLA_W
HGRN_IN_W = 3 * HGRN_K + 2 * HGRN_W


def _params(sem):
    return pltpu.CompilerParams(dimension_semantics=sem, vmem_limit_bytes=VMEM_LIMIT)


def _const_spec(shape):
    nd = len(shape)
    return pl.BlockSpec(shape, lambda *_: (0,) * nd, pipeline_mode=pl.Buffered(1))


def _dot(a, b):
    return jnp.dot(a, b, preferred_element_type=F32)


def _dot_nt(a, b):
    return lax.dot_general(a, b, (((1,), (1,)), ((), ())), preferred_element_type=F32)


def _rms(x, g):
    return x * lax.rsqrt(jnp.mean(x * x, axis=-1, keepdims=True) + EPS) * g


def _silu(x):
    return x * jax.nn.sigmoid(x)


def _rope_tile(x, cos, sin_signed):
    lane = lax.broadcasted_iota(jnp.int32, (x.shape[0], LANES), 1)
    first_half = (lane % DIFF_DQK) < (DIFF_DQK // 2)
    out = []
    for s in range(x.shape[1] // LANES):
        xs = x[:, s * LANES:(s + 1) * LANES]
        up = pltpu.roll(xs, LANES - DIFF_DQK // 2, 1)
        dn = pltpu.roll(xs, DIFF_DQK // 2, 1)
        out.append(xs * cos + jnp.where(first_half, up, dn) * sin_signed)
    return jnp.concatenate(out, axis=1)


def _in_proj_kernel(x_ref, g1_ref, w_ref, wg2_ref, bg_ref, lb_ref, cos_ref, sin_ref,
                    gla_ref, hgrn_ref, q_ref, kf_ref, vf_ref, kb_ref, vb_ref, *, layer):
    xb = _rms(x_ref[...], g1_ref[...]).astype(BF16)

    def seg(name):
        a, b = _SEG[name]
        return _dot(xb, w_ref[:, a:b])

    gg = seg("gg")
    glog = jax.nn.log_sigmoid(_dot(gg.astype(BF16), wg2_ref[...]) + bg_ref[...])
    gla_ref[:, 0:GLA_QK] = seg("gq") * (GLA_DK ** -0.5)
    gla_ref[:, GLA_QK:2 * GLA_QK] = seg("gk")
    gla_ref[:, 2 * GLA_QK:3 * GLA_QK] = glog * (1.0 / GLA_GATE_NORMALIZER)
    gla_ref[:, 3 * GLA_QK:3 * GLA_QK + GLA_W] = seg("gv")
    gla_ref[:, 3 * GLA_QK + GLA_W:] = _silu(seg("gr"))

    lb = lb_ref[...]
    lb_e = jnp.exp(lb - jnp.max(lb, axis=0, keepdims=True))
    lb_soft = lb_e / jnp.sum(lb_e, axis=0, keepdims=True)
    lower = jnp.sum(lb_soft[0:layer + 1], axis=0, keepdims=True) - lb_soft[0:1]
    f = lower + (1.0 - lower) * jax.nn.sigmoid(seg("hf"))
    hgrn_ref[:, 0:HGRN_K] = _silu(seg("hq"))
    hgrn_ref[:, HGRN_K:2 * HGRN_K] = 1.0 - f
    hgrn_ref[:, 2 * HGRN_K:3 * HGRN_K] = jnp.log(jnp.maximum(f, MIN_F))
    hgrn_ref[:, 3 * HGRN_K:3 * HGRN_K + HGRN_W] = seg("hi")
    hgrn_ref[:, 3 * HGRN_K + HGRN_W:] = _silu(seg("hg"))

    cos = cos_ref[...]
    sin = sin_ref[...]
    q = _rope_tile(seg("dq"), cos, sin) * (DIFF_DQK ** -0.5)
    k = _rope_tile(seg("dk"), cos, sin)
    v = seg("dv")
    q_ref[...] = q.astype(BF16)
    kf_ref[...] = k
    kb_ref[...] = k.astype(BF16)
    vf_ref[...] = v
    vb_ref[...] = v.astype(BF16)


def _in_proj(h, g1, w, wg2, bg, lb, cos, sin, *, layer, tm):
    m = h.shape[0]
    n_pos = cos.shape[0] // tm
    row = lambda i: (i, 0)
    out_shape = (jax.ShapeDtypeStruct((m, GLA_IN_W), F32),
                 jax.ShapeDtypeStruct((m, HGRN_IN_W), F32),
                 jax.ShapeDtypeStruct((m, DIFF_QK), BF16),
                 jax.ShapeDtypeStruct((m, DIFF_QK), F32),
                 jax.ShapeDtypeStruct((m, DIFF_W), F32),
                 jax.ShapeDtypeStruct((m, DIFF_QK), BF16),
                 jax.ShapeDtypeStruct((m, DIFF_W), BF16))
    return pl.pallas_call(
        functools.partial(_in_proj_kernel, layer=layer),
        grid=(m // tm,),
        in_specs=[pl.BlockSpec((tm, D_MODEL), row),
                  _const_spec((1, D_MODEL)),
                  _const_spec((D_MODEL, IN_W_PAD)),
                  _const_spec((LANES, GLA_QK)),
                  _const_spec((1, GLA_QK)),
                  _const_spec((DEPTH, HGRN_K)),
                  pl.BlockSpec((tm, LANES), lambda i: (i % n_pos, 0)),
                  pl.BlockSpec((tm, LANES), lambda i: (i % n_pos, 0))],
        out_specs=[pl.BlockSpec((tm, GLA_IN_W), row),
                   pl.BlockSpec((tm, HGRN_IN_W), row),
                   pl.BlockSpec((tm, DIFF_QK), row),
                   pl.BlockSpec((tm, DIFF_QK), row),
                   pl.BlockSpec((tm, DIFF_W), row),
                   pl.BlockSpec((tm, DIFF_QK), row),
                   pl.BlockSpec((tm, DIFF_W), row)],
        out_shape=out_shape,
        compiler_params=_params(("parallel",)),
        name=f"in_proj_l{layer}_m{m}",
    )(h, g1, w, wg2, bg, lb, cos, sin)


def _linrec_kernel(x_ref, s0_ref, seg_ref, seg_v_ref, gn_ref, o_ref, st_ref, s_scr,
                   *, hk, sub, rows):
    c = pl.program_id(1)
    hv = seg_ref.shape[1]

    @pl.when(c == 0)
    def _():
        s_scr[...] = s0_ref[...]

    x = x_ref[...]
    q = x[:, 0:hk]
    k = x[:, hk:2 * hk]
    g = x[:, 2 * hk:3 * hk]
    v = x[:, 3 * hk:3 * hk + hv]
    r = x[:, 3 * hk + hv:]

    pos = lax.broadcasted_iota(jnp.int32, (rows, hk), 0) % sub
    b = g
    step = 1
    while step < sub:
        b = b + jnp.where(pos >= step, pltpu.roll(b, step, 0), 0.0)
        step *= 2

    seg = seg_ref[...]
    acc = jnp.zeros((rows, hv), F32)
    for d in range(sub):
        if d == 0:
            e = q * k
            vd = v
        else:
            kd = pltpu.roll(k, d, 0)
            bd = pltpu.roll(b, d, 0)
            vd = pltpu.roll(v, d, 0)
            e = q * kd * jnp.exp(jnp.where(pos >= d, b - bd, MASK_VALUE))
        acc = acc + _dot(e.astype(BF16), seg) * vd

    n_sub = rows // sub
    b_last = jnp.concatenate(
        [jnp.broadcast_to(b[(n + 1) * sub - 1:(n + 1) * sub, :], (sub, hk)) for n in range(n_sub)],
        axis=0)
    q_dec = (q * jnp.exp(b)).astype(BF16)
    k_dec_t = (k * jnp.exp(b_last - b)).T
    dec_t = jnp.exp(b_last).T
    col = lax.broadcasted_iota(jnp.int32, (hk, rows), 1)
    vb = v.astype(BF16)
    block_diag = seg.astype(F32)
    state = s_scr[...]
    inter = []
    for n in range(n_sub):
        inter.append(_dot(q_dec[n * sub:(n + 1) * sub], state.astype(BF16)))
        in_block = (col >= n * sub) & (col < (n + 1) * sub)
        upd = _dot(jnp.where(in_block, k_dec_t, 0.0).astype(BF16), vb)
        state = state * dec_t[:, n * sub:n * sub + 1] + upd * block_diag
    s_scr[...] = state
    o = acc + jnp.concatenate(inter, axis=0)

    ms = _dot((o * o).astype(BF16), seg_v_ref[...]) * (1.0 / (hv // GLA_HEADS))
    o_ref[...] = (o * lax.rsqrt(ms + EPS) * gn_ref[...] * r).astype(BF16)

    @pl.when(c == pl.num_programs(1) - 1)
    def _():
        st_ref[...] = state


def _linrec(x, s0, gnorm, *, hk, sub, rows, name):
    bsz, t, w = x.shape
    hv = (w - 3 * hk) // 2
    heads = GLA_HEADS
    seg = (np.arange(hk)[:, None] // (hk // heads) == np.arange(hv)[None, :] // (hv // heads))
    seg_v = (np.arange(hv)[:, None] // (hv // heads) == np.arange(hv)[None, :] // (hv // heads))
    return pl.pallas_call(
        functools.partial(_linrec_kernel, hk=hk, sub=sub, rows=rows),
        grid=(bsz, t // rows),
        in_specs=[pl.BlockSpec((None, rows, w), lambda b, c: (b, c, 0)),
                  pl.BlockSpec((None, hk, hv), lambda b, c: (b, 0, 0)),
                  _const_spec((hk, hv)),
                  _const_spec((hv, hv)),
                  _const_spec((1, hv))],
        out_specs=[pl.BlockSpec((None, rows, hv), lambda b, c: (b, c, 0)),
                   pl.BlockSpec((None, hk, hv), lambda b, c: (b, 0, 0))],
        out_shape=(jax.ShapeDtypeStruct((bsz, t, hv), BF16),
                   jax.ShapeDtypeStruct((bsz, hk, hv), F32)),
        scratch_shapes=[pltpu.VMEM((hk, hv), F32)],
        compiler_params=_params(("parallel", "arbitrary")),
        name=name,
    )(x, s0, jnp.asarray(seg, BF16), jnp.asarray(seg_v, BF16), gnorm)


def _lambda(lam_ref, lam_init):
    lv = lam_ref[...]
    a = jnp.sum(lv[0:1] * lv[1:2], axis=-1, keepdims=True)
    b = jnp.sum(lv[2:3] * lv[3:4], axis=-1, keepdims=True)
    return jnp.exp(a) - jnp.exp(b) + lam_init


def _attn_prompt_kernel(q_ref, k_ref, v_ref, lam_ref, sub_ref, segv_ref, o_ref,
                        m_scr, l_scr, acc_scr, *, tq, lam_init):
    i = pl.program_id(2)
    q = q_ref[...]
    lane = lax.broadcasted_iota(jnp.int32, (tq, LANES), 1)
    qm = jnp.concatenate(
        [jnp.where(lane // DIFF_DQK == j, q, jnp.zeros_like(q)) for j in range(4)], axis=0)
    m_scr[...] = jnp.full(m_scr.shape, MASK_VALUE, F32)
    l_scr[...] = jnp.zeros(l_scr.shape, F32)
    acc_scr[...] = jnp.zeros(acc_scr.shape, F32)

    def update(kv, masked):
        start = pl.multiple_of(kv * tq, tq)
        s = _dot_nt(qm, k_ref[pl.ds(start, tq), :])
        if masked:
            rowi = lax.broadcasted_iota(jnp.int32, s.shape, 0) % tq
            coli = lax.broadcasted_iota(jnp.int32, s.shape, 1)
            s = jnp.where(coli <= rowi, s, MASK_VALUE)
        m_old = m_scr[...]
        m_new = jnp.maximum(m_old, jnp.max(s, axis=-1, keepdims=True))
        p = jnp.exp(s - m_new)
        alpha = jnp.exp(m_old - m_new)
        l_scr[...] = alpha * l_scr[...] + jnp.sum(p, axis=-1, keepdims=True)
        acc_scr[...] = alpha * acc_scr[...] + _dot(p.astype(BF16), v_ref[pl.ds(start, tq), :])
        m_scr[...] = m_new

    def body(kv, carry):
        update(kv, False)
        return carry

    lax.fori_loop(0, i, body, 0)
    update(i, True)

    lam = _lambda(lam_ref, lam_init)
    a = acc_scr[...] / l_scr[...]
    head0 = a[0:tq] - lam * a[tq:2 * tq]
    head1 = a[2 * tq:3 * tq] - lam * a[3 * tq:4 * tq]
    o = jnp.where(lane < DIFF_DV, head0, head1)
    ms = _dot((o * o).astype(BF16), segv_ref[...]) * (1.0 / DIFF_DV)
    o_ref[...] = (o * lax.rsqrt(ms + EPS) * sub_ref[...] * (1.0 - lam_init)).astype(BF16)


def _attn_prompt(q, k, v, lam_p, subln, *, bsz, t, tq, lam_init, name):
    nq = t // tq
    pairs = DIFF_W // LANES
    segv = (np.arange(LANES)[:, None] // DIFF_DV == np.arange(LANES)[None, :] // DIFF_DV)
    return pl.pallas_call(
        functools.partial(_attn_prompt_kernel, tq=tq, lam_init=lam_init),
        grid=(bsz, pairs, nq),
        in_specs=[pl.BlockSpec((tq, LANES), lambda b, h, i: (b * nq + i, h)),
                  pl.BlockSpec((t, LANES), lambda b, h, i: (b, h)),
                  pl.BlockSpec((t, LANES), lambda b, h, i: (b, h)),
                  _const_spec((4, DIFF_DQK)),
                  _const_spec((1, LANES)),
                  _const_spec((LANES, LANES))],
        out_specs=pl.BlockSpec((tq, LANES), lambda b, h, i: (b * nq + i, h)),
        out_shape=jax.ShapeDtypeStruct((bsz * t, DIFF_W), BF16),
        scratch_shapes=[pltpu.VMEM((4 * tq, 1), F32), pltpu.VMEM((4 * tq, 1), F32),
                        pltpu.VMEM((4 * tq, LANES), F32)],
        compiler_params=_params(("parallel", "parallel", "arbitrary")),
        name=name,
    )(q, k, v, lam_p, subln, jnp.asarray(segv, BF16))


DEC_PAGES_PER_STEP = 8


def _attn_decode_kernel(pt_ref, q_ref, kn_ref, vn_ref, lam_ref, sub_ref, *rest,
                        n_tok, lam_init):
    pps = DEC_PAGES_PER_STEP
    k_refs = rest[0:pps]
    v_refs = rest[pps:2 * pps]
    o_ref, m_scr, l_scr, acc_scr = rest[2 * pps:]
    step = pl.program_id(1)

    @pl.when(step == 0)
    def _():
        m_scr[...] = jnp.full(m_scr.shape, MASK_VALUE, F32)
        l_scr[...] = jnp.zeros(l_scr.shape, F32)
        acc_scr[...] = jnp.zeros(acc_scr.shape, F32)

    def update(h, s, vals_t):
        m_old = m_scr[h]
        m_new = jnp.maximum(m_old, jnp.max(s, axis=-1, keepdims=True))
        p = jnp.exp(s - m_new)
        alpha = jnp.exp(m_old - m_new)
        l_scr[h] = alpha * l_scr[h] + jnp.sum(p, axis=-1, keepdims=True)
        acc_scr[h] = alpha * acc_scr[h] + _dot_nt(p.astype(BF16), vals_t)
        m_scr[h] = m_new

    for h in range(DIFF_HEADS):
        keys_t = jnp.concatenate([r[h] for r in k_refs], axis=1).astype(BF16)
        vals_t = jnp.concatenate([r[h] for r in v_refs], axis=1).astype(BF16)
        update(h, _dot(q_ref[h].astype(BF16), keys_t), vals_t)

    @pl.when(step == pl.num_programs(1) - 1)
    def _():
        lam = _lambda(lam_ref, lam_init)
        for h in range(DIFF_HEADS):
            s = _dot(q_ref[h].astype(BF16), kn_ref[h])
            tok = lax.broadcasted_iota(jnp.int32, s.shape, 0) % n_tok
            key = lax.broadcasted_iota(jnp.int32, s.shape, 1)
            update(h, jnp.where(key <= tok, s, MASK_VALUE), vn_ref[h])
            a = acc_scr[h] / l_scr[h]
            d = a - lam * pltpu.roll(a, n_tok, 0)
            ms = jnp.mean(d * d, axis=-1, keepdims=True)
            o_ref[h] = d * lax.rsqrt(ms + EPS) * sub_ref[...] * (1.0 - lam_init)


def _attn_decode(page_table, q, k_new, v_new, lam_p, subln, cache_k_t, cache_v_t,
                 *, layer, lam_init, name):
    bsz, n_tok, _ = q.shape
    n_pages = page_table.shape[1]
    pps = DEC_PAGES_PER_STEP
    rows = 2 * n_tok
    heads = DIFF_HEADS
    qh = q.astype(F32).reshape(bsz, 1, n_tok, heads, 2, DIFF_DQK)
    comp = jnp.eye(2, dtype=F32).reshape(1, 2, 1, 1, 2, 1)
    qm = jnp.transpose(qh * comp, (0, 3, 1, 2, 4, 5)).reshape(bsz, heads, rows, 2 * DIFF_DQK)

    def new_rows(x):
        x = jnp.transpose(x.reshape(bsz, n_tok, heads, DIFF_DV), (0, 2, 3, 1))
        return jnp.pad(x, ((0, 0), (0, 0), (0, 0), (0, PAGE_SIZE - n_tok))).astype(BF16)

    q_spec = pl.BlockSpec((None, heads, rows, DIFF_DV), lambda b, s, pt: (b, 0, 0, 0))
    new_spec = pl.BlockSpec((None, heads, DIFF_DV, PAGE_SIZE), lambda b, s, pt: (b, 0, 0, 0))

    def page_spec(i):
        return pl.BlockSpec((None, None, heads, DIFF_DV, PAGE_SIZE),
                            lambda b, s, pt: (layer, pt[b, s * pps + i], 0, 0, 0))

    grid_spec = pltpu.PrefetchScalarGridSpec(
        num_scalar_prefetch=1,
        grid=(bsz, n_pages // pps),
        in_specs=[q_spec, new_spec, new_spec,
                  pl.BlockSpec((4, DIFF_DQK), lambda b, s, pt: (0, 0)),
                  pl.BlockSpec((1, DIFF_DV), lambda b, s, pt: (0, 0))]
                 + [page_spec(i) for i in range(pps)] + [page_spec(i) for i in range(pps)],
        out_specs=q_spec,
        scratch_shapes=[pltpu.VMEM((heads, rows, 1), F32), pltpu.VMEM((heads, rows, 1), F32),
                        pltpu.VMEM((heads, rows, DIFF_DV), F32)])
    o = pl.pallas_call(
        functools.partial(_attn_decode_kernel, n_tok=n_tok, lam_init=lam_init),
        grid_spec=grid_spec,
        out_shape=jax.ShapeDtypeStruct((bsz, heads, rows, DIFF_DV), F32),
        compiler_params=_params(("parallel", "arbitrary")),
        name=name,
    )(page_table, qm, new_rows(k_new), new_rows(v_new), lam_p, subln,
      *([cache_k_t] * pps), *([cache_v_t] * pps))
    o = jnp.transpose(o[:, :, :n_tok], (0, 2, 1, 3))
    return o.reshape(bsz * n_tok, DIFF_W).astype(BF16)


def _post_kernel(h_ref, oa_ref, ob_ref, oc_ref, p_ref, wo_ref, n2_ref, wup_ref, wdn_ref,
                 pg_ref, pp_ref, fn_ref, out_ref, *, final):
    h = (h_ref[...]
         + _dot(oa_ref[...], wo_ref[0:GLA_W])
         + _dot(ob_ref[...], wo_ref[GLA_W:GLA_W + HGRN_W])
         + _dot(oc_ref[...], wo_ref[GLA_W + HGRN_W:]))
    hn = _rms(h, n2_ref[...]).astype(BF16)
    up = jnp.square(jnp.maximum(_dot(hn, wup_ref[...]), 0.0)).astype(BF16)
    h = h + _dot(up, wdn_ref[...])
    gate = jax.nn.sigmoid(_dot(h.astype(BF16), pg_ref[...]))
    h = h + gate * _dot(p_ref[...].astype(BF16), pp_ref[...])
    if final:
        h = _rms(h, fn_ref[...])
    out_ref[...] = h


def _post(h, oa, ob, oc, p, wo, n2, wup, wdn, pg, pp, fn, *, final, tm, name):
    m = h.shape[0]
    row = lambda i: (i, 0)
    return pl.pallas_call(
        functools.partial(_post_kernel, final=final),
        grid=(m // tm,),
        in_specs=[pl.BlockSpec((tm, D_MODEL), row),
                  pl.BlockSpec((tm, GLA_W), row),
                  pl.BlockSpec((tm, HGRN_W), row),
                  pl.BlockSpec((tm, DIFF_W), row),
                  pl.BlockSpec((tm, PLE_DIM), row),
                  _const_spec((D_MODEL, D_MODEL)),
                  _const_spec((1, D_MODEL)),
                  _const_spec((D_MODEL, D_FF)),
                  _const_spec((D_FF, D_MODEL)),
                  _const_spec((D_MODEL, D_MODEL)),
                  _const_spec((PLE_DIM, D_MODEL)),
                  _const_spec((1, D_MODEL))],
        out_specs=pl.BlockSpec((tm, D_MODEL), row),
        out_shape=jax.ShapeDtypeStruct((m, D_MODEL), F32),
        compiler_params=_params(("parallel",)),
        name=name,
    )(h, oa, ob, oc, p, wo, n2, wup, wdn, pg, pp, fn)


def _reorder_w_in(w_in_l):
    sizes = (GLA_QK, GLA_QK, GLA_W, GLA_GATE_RANK, GLA_W, HGRN_K, HGRN_K, HGRN_W, HGRN_W,
             DIFF_QK, DIFF_QK, DIFF_W)
    names = ("gq", "gk", "gv", "gg", "gr", "hq", "hf", "hi", "hg", "dq", "dk", "dv")
    offs = np.concatenate([[0], np.cumsum(sizes)])
    cols = {n: w_in_l[:, offs[i]:offs[i + 1]] for i, n in enumerate(names)}
    cols["gg"] = jnp.pad(cols["gg"], ((0, 0), (0, LANES - GLA_GATE_RANK)))
    return jnp.concatenate([cols[n] for n, _ in _SEG_SIZES], axis=1).astype(BF16)


def _rope_tables(pos):
    half = DIFF_DQK // 2
    inv = ROPE_THETA ** (-jnp.arange(half, dtype=F32) / half)
    ang = pos.astype(F32)[:, None] * inv[None, :]
    cos = jnp.cos(ang)
    sin = jnp.sin(ang)
    reps = LANES // DIFF_DQK
    cos_t = jnp.tile(jnp.concatenate([cos, cos], axis=1), (1, reps))
    sin_t = jnp.tile(jnp.concatenate([-sin, sin], axis=1), (1, reps))
    return cos_t, sin_t


def _block_diag_state(s):
    bsz, heads, dk, dv = s.shape
    eye = jnp.eye(heads, dtype=s.dtype)
    return jnp.einsum("bhkv,hg->bhkgv", s, eye).reshape(bsz, heads * dk, heads * dv)


def _diag_blocks(s, heads):
    bsz, hk, hv = s.shape
    dk, dv = hk // heads, hv // heads
    s = s.reshape(bsz, heads, dk, heads, dv)
    return jnp.stack([s[:, h, :, h, :] for h in range(heads)], axis=1)


LINREC_SUB = 16
LINREC_ROWS = 128


def _trunk(x, p, pos_rows, gla_s0, hgrn_s0, attend, w, *, tm):
    bsz, t, _ = x.shape
    m = bsz * t
    h = x.reshape(m, D_MODEL)
    cos, sin = _rope_tables(pos_rows)
    t_pad = -(-t // LINREC_ROWS) * LINREC_ROWS
    gla_st, hgrn_st, k_rows, v_rows = [], [], [], []
    for l in range(DEPTH):
        lam_init = 0.8 - 0.6 * math.exp(-0.3 * l)
        gla_in, hgrn_in, q, kf, vf, kb, vb = _in_proj(
            h, w["norm1"][l], w["w_in"][l], w["wg2"][l], w["bg"][l], w["lb"], cos, sin,
            layer=l, tm=tm)

        def rec(xin, s0, gnorm, hk, name):
            xin = xin.reshape(bsz, t, xin.shape[-1])
            if t_pad != t:
                xin = jnp.pad(xin, ((0, 0), (0, t_pad - t), (0, 0)))
            o, st = _linrec(xin, _block_diag_state(s0), gnorm, hk=hk, sub=LINREC_SUB,
                            rows=LINREC_ROWS, name=name)
            return o[:, :t].reshape(m, o.shape[-1]), _diag_blocks(st, GLA_HEADS)

        o_a, s_a = rec(gla_in, gla_s0[l], w["gla_onorm"][l], GLA_QK, f"gla_l{l}_m{m}")
        o_b, s_b = rec(hgrn_in, hgrn_s0[l], w["hgrn_onorm"][l], HGRN_K, f"hgrn_l{l}_m{m}")
        o_c = attend(l, q, kb, vb, kf, vf, lam_init)
        h = _post(h, o_a, o_b, o_c, p[l].reshape(m, PLE_DIM), w["w_out"][l], w["norm2"][l],
                  w["w_up"][l], w["w_down"][l], w["ple_gate"][l], w["ple_proj"][l],
                  w["final_norm"], final=(l == DEPTH - 1), tm=tm, name=f"post_l{l}_m{m}")
        gla_st.append(s_a)
        hgrn_st.append(s_b)
        k_rows.append(kf.reshape(bsz, t, DIFF_HEADS, 2 * DIFF_DQK))
        v_rows.append(vf.reshape(bsz, t, DIFF_HEADS, DIFF_DV))
    return (h.reshape(bsz, t, D_MODEL), jnp.stack(gla_st), jnp.stack(hgrn_st),
            jnp.stack(k_rows), jnp.stack(v_rows))


PROMPT_TM = 512
ATTN_TQ = 256


def kernel(x_prompt, x_sample, state_gla, state_hgrn, cache_k, cache_v, page_table, p_prompt, p_sample, norm1, w_in, gla_wg2, gla_bg, gla_onorm, hgrn_lb_logits, hgrn_onorm, diff_lambda, diff_subln, w_out, norm2, w_up, w_down, ple_proj, ple_gate, final_norm):
    w = {
        "norm1": norm1.reshape(DEPTH, 1, D_MODEL),
        "w_in": [_reorder_w_in(w_in[l]) for l in range(DEPTH)],
        "wg2": jnp.pad(gla_wg2, ((0, 0), (0, LANES - GLA_GATE_RANK), (0, 0))).astype(BF16),
        "bg": gla_bg.reshape(DEPTH, 1, GLA_QK),
        "lb": hgrn_lb_logits,
        "gla_onorm": jnp.tile(gla_onorm, (1, GLA_HEADS)).reshape(DEPTH, 1, GLA_W),
        "hgrn_onorm": jnp.tile(hgrn_onorm, (1, HGRN_HEADS)).reshape(DEPTH, 1, HGRN_W),
        "w_out": w_out.astype(BF16),
        "norm2": norm2.reshape(DEPTH, 1, D_MODEL),
        "w_up": w_up.astype(BF16),
        "w_down": w_down.astype(BF16),
        "ple_gate": ple_gate.astype(BF16),
        "ple_proj": ple_proj.astype(BF16),
        "final_norm": final_norm.reshape(1, D_MODEL),
    }
    subln_pair = jnp.tile(diff_subln, (1, LANES // DIFF_DV)).reshape(DEPTH, 1, LANES)

    bsz, t = x_prompt.shape[:2]

    def attend_prompt(l, q, kb, vb, kf, vf, lam_init):
        return _attn_prompt(q, kb, vb, diff_lambda[l], subln_pair[l], bsz=bsz, t=t,
                            tq=ATTN_TQ, lam_init=lam_init, name=f"attn_prompt_l{l}")

    gla0 = jnp.zeros((DEPTH, bsz, GLA_HEADS, GLA_DK, GLA_DV), F32)
    hgrn0 = jnp.zeros((DEPTH, bsz, HGRN_HEADS, HGRN_DK, HGRN_DV), F32)
    y_p, gla_p, hgrn_p, k_p, v_p = _trunk(x_prompt, p_prompt, jnp.arange(t), gla0, hgrn0,
                                          attend_prompt, w, tm=PROMPT_TM)

    bd, td = x_sample.shape[:2]
    past = page_table.shape[1] * PAGE_SIZE
    ck = jnp.transpose(cache_k, (0, 1, 3, 4, 2))
    cv = jnp.transpose(cache_v, (0, 1, 3, 4, 2))
    subln_head = diff_subln.reshape(DEPTH, 1, DIFF_DV)

    def attend_sample(l, q, kb, vb, kf, vf, lam_init):
        return _attn_decode(page_table, q.reshape(bd, td, DIFF_QK), kf.reshape(bd, td, DIFF_QK),
                            vf.reshape(bd, td, DIFF_W), diff_lambda[l], subln_head[l], ck, cv,
                            layer=l, lam_init=lam_init, name=f"attn_decode_l{l}")

    pos_s = past + jnp.tile(jnp.arange(td), bd)
    y_s, gla_s, hgrn_s, k_s, v_s = _trunk(x_sample, p_sample, pos_s, state_gla, state_hgrn,
                                          attend_sample, w, tm=bd * td)
    return (y_p, y_s, gla_p, hgrn_p, k_p, v_p, gla_s, hgrn_s, k_s, v_s)
```
